```python
import math
import jax, jax.numpy as jnp
from jax import lax
import numpy as np

D_MODEL = 1024
BATCH = 8
SEQ = 4096
DEPTH = 4

ATT_HEADS = 8
HEAD_DIM = 64
ATT_WIDTH = ATT_HEADS * HEAD_DIM
CONV_GROUPS = 8
CONV_WIDTH = D_MODEL - ATT_WIDTH
CONV_K = 3
FFN_CONV_K = 3
D_FF = 2816
Q_BLOCK = 128
RMS_EPS = 1e-6
IN_COLS = 3 * ATT_WIDTH + 3 * CONV_WIDTH + 2 * D_MODEL
IN_SPLITS = (ATT_WIDTH, 2 * ATT_WIDTH, 3 * ATT_WIDTH,
             3 * ATT_WIDTH + CONV_WIDTH, 3 * ATT_WIDTH + 2 * CONV_WIDTH,
             3 * ATT_WIDTH + 3 * CONV_WIDTH, 3 * ATT_WIDTH + 3 * CONV_WIDTH + D_MODEL)

kernel_name = "hybrid_stickbreak_shortconv_convffn"


def rms_norm(x, g):
    xf = x.astype(jnp.float32)
    var = jnp.mean(xf * xf, axis=-1, keepdims=True)
    return (xf * lax.rsqrt(var + RMS_EPS) * g.astype(jnp.float32)).astype(x.dtype)


def causal_dwconv(u, w):
    kw = w.shape[0]
    s = u.shape[1]
    up = jnp.pad(u, ((0, 0), (kw - 1, 0), (0, 0)))
    y = up[:, 0:s, :] * w[0]
    for i in range(1, kw):
        y = y + up[:, i:i + s, :] * w[i]
    return y


def stick_breaking_attention(q, k, v):
    _, _, s, dh = q.shape
    scale = 1.0 / math.sqrt(dh)
    outs = []
    for blk in range(s // Q_BLOCK):
        t0 = blk * Q_BLOCK
        t1 = t0 + Q_BLOCK
        qb = q[:, :, t0:t1]
        kb = k[:, :, :t1]
        vb = v[:, :, :t1]
        z = jnp.einsum("bhqd,bhkd->bhqk", qb, kb).astype(jnp.float32) * scale
        t_idx = t0 + jnp.arange(Q_BLOCK)[:, None]
        s_idx = jnp.arange(t1)[None, :]
        strict = s_idx < t_idx
        log_keep = jnp.where(strict, jax.nn.log_sigmoid(-z), 0.0)
        later = lax.cumsum(log_keep, axis=3, reverse=True) - log_keep
        w = jnp.where(strict, jnp.exp(jax.nn.log_sigmoid(z) + later), 0.0)
        outs.append(jnp.einsum("bhqk,bhkd->bhqd", w, vb.astype(jnp.float32)))
    return jnp.concatenate(outs, axis=2).astype(q.dtype)


def setup_inputs(seed: int = 0) -> dict:
    key = jax.random.key(seed)
    ks = jax.random.split(key, 14)
    f32 = jnp.float32

    def nrm(k, shape, fan_in):
        return jax.random.normal(k, shape, f32) * (fan_in ** -0.5)

    def gain(k):
        return 1.0 + 0.05 * jax.random.normal(k, (DEPTH, D_MODEL), f32)

    return {
        "x": jax.random.normal(ks[0], (BATCH, SEQ, D_MODEL), f32),
        "norm_mix_pre": gain(ks[1]),
        "w_in": nrm(ks[2], (DEPTH, D_MODEL, IN_COLS), D_MODEL),
        "conv_mix_w": nrm(ks[3], (DEPTH, CONV_K, CONV_WIDTH), CONV_K),
        "w_att_branch": nrm(ks[4], (DEPTH, ATT_WIDTH, D_MODEL), ATT_WIDTH),
        "w_conv_branch": nrm(ks[5], (DEPTH, CONV_WIDTH, D_MODEL), CONV_WIDTH),
        "w_out": nrm(ks[6], (DEPTH, D_MODEL, D_MODEL), D_MODEL),
        "norm_mix_post": gain(ks[7]),
        "norm_ffn_pre": gain(ks[8]),
        "w_up": nrm(ks[9], (DEPTH, D_MODEL, 2 * D_FF), D_MODEL),
        "conv_ffn_w": nrm(ks[10], (DEPTH, FFN_CONV_K, 2 * D_FF), FFN_CONV_K),
        "w_down": nrm(ks[11], (DEPTH, D_FF, D_MODEL), D_FF),
        "norm_ffn_post": gain(ks[12]),
    }


def reference(x, norm_mix_pre, w_in, conv_mix_w, w_att_branch, w_conv_branch, w_out,
              norm_mix_post, norm_ffn_pre, w_up, conv_ffn_w, w_down, norm_ffn_post):
    b, s, _ = x.shape
    for l in range(DEPTH):
        h = rms_norm(x, norm_mix_pre[l])
        proj = h @ w_in[l]
        q, k, v, cb, cc, cx, g_att, g_conv = jnp.split(proj, IN_SPLITS, axis=-1)

        def heads(t):
            return t.reshape(b, s, ATT_HEADS, HEAD_DIM).transpose(0, 2, 1, 3)

        o = stick_breaking_attention(heads(q), heads(k), heads(v))
        y_att = o.transpose(0, 2, 1, 3).reshape(b, s, ATT_WIDTH) @ w_att_branch[l]

        y_conv = (cb * causal_dwconv(cc * cx, conv_mix_w[l])) @ w_conv_branch[l]

        merged = jax.nn.sigmoid(g_att) * y_att + jax.nn.sigmoid(g_conv) * y_conv
        x = x + rms_norm(merged @ w_out[l], norm_mix_post[l])

        h = rms_norm(x, norm_ffn_pre[l])
        u = causal_dwconv(h @ w_up[l], conv_ffn_w[l])
        a, g = jnp.split(u, 2, axis=-1)
        f = (jax.nn.gelu(g, approximate=True) * a) @ w_down[l]
        x = x + rms_norm(f, norm_ffn_post[l])
    return x
```

```python
import functools

import jax
import jax.numpy as jnp
from jax import lax
from jax.experimental import pallas as pl
from jax.experimental.pallas import tpu as pltpu

D_MODEL = 1024
ATT_HEADS = 8
HEAD_DIM = 64
ATT_WIDTH = ATT_HEADS * HEAD_DIM
CONV_WIDTH = D_MODEL - ATT_WIDTH
CONV_K = 3
D_FF = 2816
RMS_EPS = 1e-6
IN_COLS = 3 * ATT_WIDTH + 3 * CONV_WIDTH + 2 * D_MODEL

LANES = 128
SUBLANES = 8
VMEM_LIMIT_BYTES = 56 * 1024 * 1024

TM_PROJ = 512
TM_FFN = 512
TQ = 256
TK = 256
FF_CHUNK = 256
HEADS_PER_STEP = LANES // HEAD_DIM

BF16 = jnp.bfloat16
F32 = jnp.float32


def _rms_norm(x, g):
    var = jnp.mean(x * x, axis=-1, keepdims=True)
    return x * lax.rsqrt(var + RMS_EPS) * g


def _causal_conv3(u, w_ref, c0, c1, carry_ref, ubuf_ref):
    tm = u.shape[0]
    width = c1 - c0
    ubuf_ref[0:SUBLANES, 0:width] = carry_ref[:, c0:c1]
    ubuf_ref[SUBLANES:SUBLANES + tm, 0:width] = u
    carry_ref[:, c0:c1] = u[tm - SUBLANES:tm, :]
    u1 = ubuf_ref[SUBLANES - 1:SUBLANES - 1 + tm, 0:width]
    u2 = ubuf_ref[SUBLANES - 2:SUBLANES - 2 + tm, 0:width]
    return (w_ref[0:1, c0:c1] * u2 + w_ref[1:2, c0:c1] * u1) + w_ref[2:3, c0:c1] * u


def _in_proj_kernel(x_ref, g_ref, w_ref, cw_ref, qkv_ref, uc_ref, gate_ref,
                    carry_ref, ubuf_ref):
    @pl.when(pl.program_id(1) == 0)
    def _():
        carry_ref[...] = jnp.zeros_like(carry_ref)

    h = _rms_norm(x_ref[0], g_ref[...]).astype(BF16)

    def proj(c0, c1):
        return jnp.dot(h, w_ref[:, c0:c1], preferred_element_type=F32)

    w = ATT_WIDTH
    qkv_ref[0, :, 0:w] = (proj(0, w) * (HEAD_DIM ** -0.5)).astype(BF16)
    qkv_ref[0, :, w:2 * w] = proj(w, 2 * w).astype(BF16)
    qkv_ref[0, :, 2 * w:3 * w] = proj(2 * w, 3 * w).astype(BF16)

    base = 3 * ATT_WIDTH
    cw = CONV_WIDTH
    cb = proj(base, base + cw)
    cc = proj(base + cw, base + 2 * cw)
    cx = proj(base + 2 * cw, base + 3 * cw)
    y = _causal_conv3(cc * cx, cw_ref, 0, cw, carry_ref, ubuf_ref)
    uc_ref[0] = (cb * y).astype(BF16)

    gbase = base + 3 * cw
    for c in range(2 * D_MODEL // cw):
        p = proj(gbase + c * cw, gbase + (c + 1) * cw)
        gate_ref[0, :, c * cw:(c + 1) * cw] = jax.nn.sigmoid(p).astype(BF16)


def _in_proj(x, g, w_in, conv_w):
    b, s, d = x.shape
    tm = TM_PROJ
    const = lambda *_: (0, 0)
    return pl.pallas_call(
        _in_proj_kernel,
        name="in_proj",
        grid=(b, s // tm),
        in_specs=[
            pl.BlockSpec((1, tm, d), lambda i, j: (i, j, 0)),
            pl.BlockSpec((1, d), const),
            pl.BlockSpec((d, IN_COLS), const),
            pl.BlockSpec((CONV_K, CONV_WIDTH), const),
        ],
        out_specs=[
            pl.BlockSpec((1, tm, 3 * ATT_WIDTH), lambda i, j: (i, j, 0)),
            pl.BlockSpec((1, tm, CONV_WIDTH), lambda i, j: (i, j, 0)),
            pl.BlockSpec((1, tm, 2 * D_MODEL), lambda i, j: (i, j, 0)),
        ],
        out_shape=[
            jax.ShapeDtypeStruct((b, s, 3 * ATT_WIDTH), BF16),
            jax.ShapeDtypeStruct((b, s, CONV_WIDTH), BF16),
            jax.ShapeDtypeStruct((b, s, 2 * D_MODEL), BF16),
        ],
        scratch_shapes=[
            pltpu.VMEM((SUBLANES, CONV_WIDTH), F32),
            pltpu.VMEM((SUBLANES + tm, CONV_WIDTH), F32),
        ],
        compiler_params=pltpu.CompilerParams(
            dimension_semantics=("arbitrary", "arbitrary"),
            vmem_limit_bytes=VMEM_LIMIT_BYTES),
    )(x, g, w_in, conv_w)


def _attn_kernel(q_ref, k_ref, v_ref, o_ref):
    i = pl.program_id(2)
    lane = lax.broadcasted_iota(jnp.int32, (1, LANES), 1)
    row = lax.broadcasted_iota(jnp.int32, (TK, TK), 0)
    col = lax.broadcasted_iota(jnp.int32, (TK, TK), 1)
    suffix = (row >= col).astype(BF16)
    strict = col < row
    q = q_ref[0]

    def block(j, qh, carry, diagonal):
        acc, run = carry
        start = pl.multiple_of(j * TK, TK)
        ks = k_ref[0, pl.ds(start, TK), :]
        vs = v_ref[0, pl.ds(start, TK), :]
        z = lax.dot_general(qh, ks, (((1,), (1,)), ((), ())), preferred_element_type=F32)
        log_keep = -(jnp.maximum(z, 0.0) + jnp.log1p(jnp.exp(-jnp.abs(z))))
        if diagonal:
            log_keep = jnp.where(strict, log_keep, 0.0)
        incl = jnp.dot(log_keep.astype(BF16), suffix, preferred_element_type=F32)
        w = jnp.exp(z + incl + run)
        if diagonal:
            w = jnp.where(strict, w, 0.0)
        acc = acc + jnp.dot(w.astype(BF16), vs, preferred_element_type=F32)
        return acc, run + incl[:, 0:1]

    outs = []
    for hh in range(HEADS_PER_STEP):
        qh = jnp.where(lane // HEAD_DIM == hh, q, jnp.zeros_like(q))
        carry = (jnp.zeros((TQ, LANES), F32), jnp.zeros((TQ, 1), F32))
        carry = block(i, qh, carry, True)
        carry = lax.fori_loop(
            0, i, lambda jj, c, qh=qh: block(i - 1 - jj, qh, c, False), carry)
        outs.append(carry[0])
    o_ref[0] = jnp.where(lane < HEAD_DIM, outs[0], outs[1]).astype(o_ref.dtype)


def _attention(qkv):
    b, s, _ = qkv.shape
    assert TQ == TK and s % TQ == 0
    n_pairs = ATT_WIDTH // LANES
    return pl.pallas_call(
        _attn_kernel,
        name="attn",
        grid=(b, n_pairs, s // TQ),
        in_specs=[
            pl.BlockSpec((1, TQ, LANES), lambda bi, p, i: (bi, i, p)),
            pl.BlockSpec((1, s, LANES), lambda bi, p, i: (bi, 0, n_pairs + p)),
            pl.BlockSpec((1, s, LANES), lambda bi, p, i: (bi, 0, 2 * n_pairs + p)),
        ],
        out_specs=pl.BlockSpec((1, TQ, LANES), lambda bi, p, i: (bi, i, p)),
        out_shape=jax.ShapeDtypeStruct((b, s, ATT_WIDTH), BF16),
        compiler_params=pltpu.CompilerParams(
            dimension_semantics=("arbitrary", "arbitrary", "arbitrary"),
            vmem_limit_bytes=VMEM_LIMIT_BYTES),
    )(qkv, qkv, qkv)


def _mix_out_kernel(o_ref, uc_ref, gate_ref, x_ref, pa_ref, pc_ref, wo_ref, g_ref, out_ref):
    y_att = jnp.dot(o_ref[0], pa_ref[...], preferred_element_type=F32)
    y_conv = jnp.dot(uc_ref[0], pc_ref[...], preferred_element_type=F32)
    g_att = gate_ref[0, :, 0:D_MODEL].astype(F32)
    g_conv = gate_ref[0, :, D_MODEL:2 * D_MODEL].astype(F32)
    merged = (g_att * y_att + g_conv * y_conv).astype(BF16)
    r = jnp.dot(merged, wo_ref[...], preferred_element_type=F32)
    out_ref[0] = x_ref[0] + _rms_norm(r, g_ref[...])


def _mix_out(o, uc, gates, x, p_att, p_conv, w_out, g):
    b, s, d = x.shape
    tm = TM_PROJ
    const = lambda *_: (0, 0)
    tile = lambda width: pl.BlockSpec((1, tm, width), lambda i, j: (i, j, 0))
    return pl.pallas_call(
        _mix_out_kernel,
        name="mix_out",
        grid=(b, s // tm),
        in_specs=[
            tile(ATT_WIDTH), tile(CONV_WIDTH), tile(2 * D_MODEL), tile(d),
            pl.BlockSpec((ATT_WIDTH, d), const),
            pl.BlockSpec((CONV_WIDTH, d), const),
            pl.BlockSpec((d, d), const),
            pl.BlockSpec((1, d), const),
        ],
        out_specs=tile(d),
        out_shape=jax.ShapeDtypeStruct((b, s, d), F32),
        input_output_aliases={3: 0},
        compiler_params=pltpu.CompilerParams(
            dimension_semantics=("arbitrary", "arbitrary"),
            vmem_limit_bytes=VMEM_LIMIT_BYTES),
    )(o, uc, gates, x, p_att, p_conv, w_out, g)


def _ffn_kernel(x_ref, gpre_ref, wup_ref, cw_ref, wdn_ref, gpost_ref, out_ref,
                carry_ref, ubuf_ref, f_ref):
    @pl.when(pl.program_id(1) == 0)
    def _():
        carry_ref[...] = jnp.zeros_like(carry_ref)

    x = x_ref[0]
    h = _rms_norm(x, gpre_ref[...]).astype(BF16)
    for c in range(D_FF // FF_CHUNK):
        halves = []
        for base in (0, D_FF):
            c0 = base + c * FF_CHUNK
            c1 = c0 + FF_CHUNK
            u = jnp.dot(h, wup_ref[:, c0:c1], preferred_element_type=F32)
            halves.append(_causal_conv3(u, cw_ref, c0, c1, carry_ref, ubuf_ref))
        a, g = halves
        f_ref[:, c * FF_CHUNK:(c + 1) * FF_CHUNK] = (
            jax.nn.gelu(g, approximate=True) * a).astype(BF16)
    f = jnp.dot(f_ref[...], wdn_ref[...], preferred_element_type=F32)
    out_ref[0] = x + _rms_norm(f, gpost_ref[...])


def _ffn(x, g_pre, w_up, conv_w, w_down, g_post):
    b, s, d = x.shape
    tm = TM_FFN
    const = lambda *_: (0, 0)
    resident = lambda shape: pl.BlockSpec(shape, const, pipeline_mode=pl.Buffered(1))
    return pl.pallas_call(
        _ffn_kernel,
        name="ffn",
        grid=(b, s // tm),
        in_specs=[
            pl.BlockSpec((1, tm, d), lambda i, j: (i, j, 0)),
            pl.BlockSpec((1, d), const),
            resident((d, 2 * D_FF)),
            pl.BlockSpec((CONV_K, 2 * D_FF), const),
            resident((D_FF, d)),
            pl.BlockSpec((1, d), const),
        ],
        out_specs=pl.BlockSpec((1, tm, d), lambda i, j: (i, j, 0)),
        out_shape=jax.ShapeDtypeStruct((b, s, d), F32),
        input_output_aliases={0: 0},
        scratch_shapes=[
            pltpu.VMEM((SUBLANES, 2 * D_FF), F32),
            pltpu.VMEM((SUBLANES + tm, FF_CHUNK), F32),
            pltpu.VMEM((tm, D_FF), BF16),
        ],
        compiler_params=pltpu.CompilerParams(
            dimension_semantics=("arbitrary", "arbitrary"),
            vmem_limit_bytes=VMEM_LIMIT_BYTES),
    )(x, g_pre, w_up, conv_w, w_down, g_post)


def kernel(x, norm_mix_pre, w_in, conv_mix_w, w_att_branch, w_conv_branch, w_out,
           norm_mix_post, norm_ffn_pre, w_up, conv_ffn_w, w_down, norm_ffn_post):
    depth = w_in.shape[0]
    w_in, w_att_branch, w_conv_branch, w_out, w_up, w_down = (
        w.astype(BF16) for w in (w_in, w_att_branch, w_conv_branch, w_out, w_up, w_down))
    row = lambda g, l: g[l][None, :]
    for l in range(depth):
        qkv, uc, gates = _in_proj(x, row(norm_mix_pre, l), w_in[l], conv_mix_w[l])
        o = _attention(qkv)
        x = _mix_out(o, uc, gates, x, w_att_branch[l], w_conv_branch[l], w_out[l],
                     row(norm_mix_post, l))
        x = _ffn(x, row(norm_ffn_pre, l), w_up[l], conv_ffn_w[l], w_down[l],
                 row(norm_ffn_post, l))
    return x
```

```python
import functools

import jax
import jax.numpy as jnp
from jax import lax
from jax.experimental import pallas as pl
from jax.experimental.pallas import tpu as pltpu

D_MODEL = 1024
ATT_HEADS = 8
HEAD_DIM = 64
ATT_WIDTH = ATT_HEADS * HEAD_DIM
CONV_WIDTH = D_MODEL - ATT_WIDTH
CONV_K = 3
D_FF = 2816
RMS_EPS = 1e-6
IN_COLS = 3 * ATT_WIDTH + 3 * CONV_WIDTH + 2 * D_MODEL

LANES = 128
SUBLANES = 8
VMEM_LIMIT_BYTES = 56 * 1024 * 1024

TM_PROJ = 512
TM_FFN = 512
TQ = 256
TK = 256
FF_CHUNK = 256
HEADS_PER_STEP = LANES // HEAD_DIM
EXP_ZERO_BELOW = -104.0

BF16 = jnp.bfloat16
F32 = jnp.float32


def _rms_norm(x, g):
    var = jnp.mean(x * x, axis=-1, keepdims=True)
    return x * lax.rsqrt(var + RMS_EPS) * g


def _causal_conv3(u, w_ref, c0, c1, carry_ref, ubuf_ref):
    tm = u.shape[0]
    width = c1 - c0
    ubuf_ref[0:SUBLANES, 0:width] = carry_ref[:, c0:c1]
    ubuf_ref[SUBLANES:SUBLANES + tm, 0:width] = u
    carry_ref[:, c0:c1] = u[tm - SUBLANES:tm, :]
    u1 = ubuf_ref[SUBLANES - 1:SUBLANES - 1 + tm, 0:width]
    u2 = ubuf_ref[SUBLANES - 2:SUBLANES - 2 + tm, 0:width]
    return (w_ref[0:1, c0:c1] * u2 + w_ref[1:2, c0:c1] * u1) + w_ref[2:3, c0:c1] * u


def _in_proj_kernel(x_ref, g_ref, w_ref, cw_ref, qkv_ref, uc_ref, gate_ref,
                    carry_ref, ubuf_ref):
    @pl.when(pl.program_id(1) == 0)
    def _():
        carry_ref[...] = jnp.zeros_like(carry_ref)

    h = _rms_norm(x_ref[0], g_ref[...]).astype(BF16)

    def proj(c0, c1):
        return jnp.dot(h, w_ref[:, c0:c1], preferred_element_type=F32)

    w = ATT_WIDTH
    qkv_ref[0, :, 0:w] = (proj(0, w) * (HEAD_DIM ** -0.5)).astype(BF16)
    qkv_ref[0, :, w:2 * w] = proj(w, 2 * w).astype(BF16)
    qkv_ref[0, :, 2 * w:3 * w] = proj(2 * w, 3 * w).astype(BF16)

    base = 3 * ATT_WIDTH
    cw = CONV_WIDTH
    cb = proj(base, base + cw)
    cc = proj(base + cw, base + 2 * cw)
    cx = proj(base + 2 * cw, base + 3 * cw)
    y = _causal_conv3(cc * cx, cw_ref, 0, cw, carry_ref, ubuf_ref)
    uc_ref[0] = (cb * y).astype(BF16)

    gbase = base + 3 * cw
    for c in range(2 * D_MODEL // cw):
        p = proj(gbase + c * cw, gbase + (c + 1) * cw)
        gate_ref[0, :, c * cw:(c + 1) * cw] = jax.nn.sigmoid(p).astype(BF16)


def _in_proj(x, g, w_in, conv_w):
    b, s, d = x.shape
    tm = TM_PROJ
    const = lambda *_: (0, 0)
    return pl.pallas_call(
        _in_proj_kernel,
        name="in_proj",
        grid=(b, s // tm),
        in_specs=[
            pl.BlockSpec((1, tm, d), lambda i, j: (i, j, 0)),
            pl.BlockSpec((1, d), const),
            pl.BlockSpec((d, IN_COLS), const),
            pl.BlockSpec((CONV_K, CONV_WIDTH), const),
        ],
        out_specs=[
            pl.BlockSpec((1, tm, 3 * ATT_WIDTH), lambda i, j: (i, j, 0)),
            pl.BlockSpec((1, tm, CONV_WIDTH), lambda i, j: (i, j, 0)),
            pl.BlockSpec((1, tm, 2 * D_MODEL), lambda i, j: (i, j, 0)),
        ],
        out_shape=[
            jax.ShapeDtypeStruct((b, s, 3 * ATT_WIDTH), BF16),
            jax.ShapeDtypeStruct((b, s, CONV_WIDTH), BF16),
            jax.ShapeDtypeStruct((b, s, 2 * D_MODEL), BF16),
        ],
        scratch_shapes=[
            pltpu.VMEM((SUBLANES, CONV_WIDTH), F32),
            pltpu.VMEM((SUBLANES + tm, CONV_WIDTH), F32),
        ],
        compiler_params=pltpu.CompilerParams(
            dimension_semantics=("arbitrary", "arbitrary"),
            vmem_limit_bytes=VMEM_LIMIT_BYTES),
    )(x, g, w_in, conv_w)


def _attn_kernel(q_ref, k_ref, v_ref, o_ref):
    i = pl.program_id(2)
    lane = lax.broadcasted_iota(jnp.int32, (1, LANES), 1)
    row = lax.broadcasted_iota(jnp.int32, (TK, TK), 0)
    col = lax.broadcasted_iota(jnp.int32, (TK, TK), 1)
    neg_after = jnp.where(row > col, -1.0, 0.0).astype(BF16)
    strict = col < row
    head0 = lane < HEAD_DIM
    q = q_ref[0]
    qs = (jnp.where(head0, q, jnp.zeros_like(q)), jnp.where(head0, jnp.zeros_like(q), q))

    def block(j, acc, runs, diagonal):
        start = pl.multiple_of(j * TK, TK)
        ks = k_ref[0, pl.ds(start, TK), :]
        vs = v_ref[0, pl.ds(start, TK), :]
        zero = jnp.zeros_like(vs)
        v_stack = jnp.concatenate([jnp.where(head0, vs, zero), jnp.where(head0, zero, vs)], axis=0)
        ws, new_runs = [], []
        for qh, run in zip(qs, runs):
            z = lax.dot_general(qh, ks, (((1,), (1,)), ((), ())), preferred_element_type=F32)
            softplus = jnp.maximum(z, 0.0) + jnp.log(1.0 + jnp.exp(-jnp.abs(z)))
            if diagonal:
                softplus = jnp.where(strict, softplus, 0.0)
            later = jnp.dot(softplus.astype(BF16), neg_after, preferred_element_type=F32)
            w = jnp.exp((z - softplus) + later + run)
            if diagonal:
                w = jnp.where(strict, w, 0.0)
            ws.append(w.astype(BF16))
            new_runs.append(run + (later[:, 0:1] - softplus[:, 0:1]))
        acc = acc + jnp.dot(jnp.concatenate(ws, axis=1), v_stack, preferred_element_type=F32)
        return acc, new_runs

    def any_alive(runs):
        alive = jnp.max(jnp.maximum(runs[0], runs[1])) > EXP_ZERO_BELOW
        return alive.astype(jnp.int32)

    acc = jnp.zeros((TQ, LANES), F32)
    runs = [jnp.zeros((TQ, 1), F32)] * HEADS_PER_STEP
    acc, runs = block(i, acc, runs, True)

    def cond(c):
        return jnp.logical_and(c[0] >= 0, c[1] > 0)

    def body(c):
        j, _, acc, r0, r1 = c
        acc, (r0, r1) = block(j, acc, [r0, r1], False)
        return j - 1, any_alive([r0, r1]), acc, r0, r1

    c = lax.while_loop(cond, body, (i - 1, any_alive(runs), acc, runs[0], runs[1]))
    o_ref[0] = c[2].astype(o_ref.dtype)


def _attention(qkv):
    b, s, _ = qkv.shape
    assert TQ == TK and s % TQ == 0
    n_pairs = ATT_WIDTH // LANES
    return pl.pallas_call(
        _attn_kernel,
        name="attn",
        grid=(b, n_pairs, s // TQ),
        in_specs=[
            pl.BlockSpec((1, TQ, LANES), lambda bi, p, i: (bi, i, p)),
            pl.BlockSpec((1, s, LANES), lambda bi, p, i: (bi, 0, n_pairs + p)),
            pl.BlockSpec((1, s, LANES), lambda bi, p, i: (bi, 0, 2 * n_pairs + p)),
        ],
        out_specs=pl.BlockSpec((1, TQ, LANES), lambda bi, p, i: (bi, i, p)),
        out_shape=jax.ShapeDtypeStruct((b, s, ATT_WIDTH), BF16),
        compiler_params=pltpu.CompilerParams(
            dimension_semantics=("arbitrary", "arbitrary", "arbitrary"),
            vmem_limit_bytes=VMEM_LIMIT_BYTES),
    )(qkv, qkv, qkv)


def _mix_out_kernel(o_ref, uc_ref, gate_ref, x_ref, pa_ref, pc_ref, wo_ref, g_ref, out_ref):
    y_att = jnp.dot(o_ref[0], pa_ref[...], preferred_element_type=F32)
    y_conv = jnp.dot(uc_ref[0], pc_ref[...], preferred_element_type=F32)
    g_att = gate_ref[0, :, 0:D_MODEL].astype(F32)
    g_conv = gate_ref[0, :, D_MODEL:2 * D_MODEL].astype(F32)
    merged = (g_att * y_att + g_conv * y_conv).astype(BF16)
    r = jnp.dot(merged, wo_ref[...], preferred_element_type=F32)
    out_ref[0] = x_ref[0] + _rms_norm(r, g_ref[...])


def _mix_out(o, uc, gates, x, p_att, p_conv, w_out, g):
    b, s, d = x.shape
    tm = TM_PROJ
    const = lambda *_: (0, 0)
    tile = lambda width: pl.BlockSpec((1, tm, width), lambda i, j: (i, j, 0))
    return pl.pallas_call(
        _mix_out_kernel,
        name="mix_out",
        grid=(b, s // tm),
        in_specs=[
            tile(ATT_WIDTH), tile(CONV_WIDTH), tile(2 * D_MODEL), tile(d),
            pl.BlockSpec((ATT_WIDTH, d), const),
            pl.BlockSpec((CONV_WIDTH, d), const),
            pl.BlockSpec((d, d), const),
            pl.BlockSpec((1, d), const),
        ],
        out_specs=tile(d),
        out_shape=jax.ShapeDtypeStruct((b, s, d), F32),
        input_output_aliases={3: 0},
        compiler_params=pltpu.CompilerParams(
            dimension_semantics=("arbitrary", "arbitrary"),
            vmem_limit_bytes=VMEM_LIMIT_BYTES),
    )(o, uc, gates, x, p_att, p_conv, w_out, g)


def _ffn_kernel(x_ref, gpre_ref, wup_ref, cw_ref, wdn_ref, gpost_ref, out_ref,
                carry_ref, ubuf_ref, f_ref):
    @pl.when(pl.program_id(1) == 0)
    def _():
        carry_ref[...] = jnp.zeros_like(carry_ref)

    x = x_ref[0]
    h = _rms_norm(x, gpre_ref[...]).astype(BF16)
    for c in range(D_FF // FF_CHUNK):
        halves = []
        for base in (0, D_FF):
            c0 = base + c * FF_CHUNK
            c1 = c0 + FF_CHUNK
            u = jnp.dot(h, wup_ref[:, c0:c1], preferred_element_type=F32)
            halves.append(_causal_conv3(u, cw_ref, c0, c1, carry_ref, ubuf_ref))
        a, g = halves
        f_ref[:, c * FF_CHUNK:(c + 1) * FF_CHUNK] = (
            jax.nn.gelu(g, approximate=True) * a).astype(BF16)
    f = jnp.dot(f_ref[...], wdn_ref[...], preferred_element_type=F32)
    out_ref[0] = x + _rms_norm(f, gpost_ref[...])


def _ffn(x, g_pre, w_up, conv_w, w_down, g_post):
    b, s, d = x.shape
    tm = TM_FFN
    const = lambda *_: (0, 0)
    resident = lambda shape: pl.BlockSpec(shape, const, pipeline_mode=pl.Buffered(1))
    return pl.pallas_call(
        _ffn_kernel,
        name="ffn",
        grid=(b, s // tm),
        in_specs=[
            pl.BlockSpec((1, tm, d), lambda i, j: (i, j, 0)),
            pl.BlockSpec((1, d), const),
            resident((d, 2 * D_FF)),
            pl.BlockSpec((CONV_K, 2 * D_FF), const),
            resident((D_FF, d)),
            pl.BlockSpec((1, d), const),
        ],
        out_specs=pl.BlockSpec((1, tm, d), lambda i, j: (i, j, 0)),
        out_shape=jax.ShapeDtypeStruct((b, s, d), F32),
        input_output_aliases={0: 0},
        scratch_shapes=[
            pltpu.VMEM((SUBLANES, 2 * D_FF), F32),
            pltpu.VMEM((SUBLANES + tm, FF_CHUNK), F32),
            pltpu.VMEM((tm, D_FF), BF16),
        ],
        compiler_params=pltpu.CompilerParams(
            dimension_semantics=("arbitrary", "arbitrary"),
            vmem_limit_bytes=VMEM_LIMIT_BYTES),
    )(x, g_pre, w_up, conv_w, w_down, g_post)


def kernel(x, norm_mix_pre, w_in, conv_mix_w, w_att_branch, w_conv_branch, w_out,
           norm_mix_post, norm_ffn_pre, w_up, conv_ffn_w, w_down, norm_ffn_post):
    depth = w_in.shape[0]
    w_in, w_att_branch, w_conv_branch, w_out, w_up, w_down = (
        w.astype(BF16) for w in (w_in, w_att_branch, w_conv_branch, w_out, w_up, w_down))
    row = lambda g, l: g[l][None, :]
    for l in range(depth):
        qkv, uc, gates = _in_proj(x, row(norm_mix_pre, l), w_in[l], conv_mix_w[l])
        o = _attention(qkv)
        x = _mix_out(o, uc, gates, x, w_att_branch[l], w_conv_branch[l], w_out[l],
                     row(norm_mix_post, l))
        x = _ffn(x, row(norm_ffn_pre, l), w_up[l], conv_ffn_w[l], w_down[l],
                 row(norm_ffn_post, l))
    return x
```

```python
import functools

import jax
import jax.numpy as jnp
from jax import lax
from jax.experimental import pallas as pl
from jax.experimental.pallas import tpu as pltpu

D_MODEL = 1024
ATT_HEADS = 8
HEAD_DIM = 64
ATT_WIDTH = ATT_HEADS * HEAD_DIM
CONV_WIDTH = D_MODEL - ATT_WIDTH
CONV_K = 3
D_FF = 2816
RMS_EPS = 1e-6
IN_COLS = 3 * ATT_WIDTH + 3 * CONV_WIDTH + 2 * D_MODEL

LANES = 128
SUBLANES = 8
VMEM_LIMIT_BYTES = 56 * 1024 * 1024

TM_PROJ = 1024
TM_FFN = 1024
TQ = 256
TK = 256
FF_CHUNK = 256
HEADS_PER_PAIR = LANES // HEAD_DIM
PAIRS_PER_STEP = 4
EXP_ZERO_BELOW = -104.0
NO_BLOCK_RUN = -1e30
LOG2_E = 1.4426950408889634

BF16 = jnp.bfloat16
F32 = jnp.float32


def _rms_norm(x, g):
    var = jnp.mean(x * x, axis=-1, keepdims=True)
    return x * lax.rsqrt(var + RMS_EPS) * g


def _causal_conv3(u, w_ref, c0, c1, carry_ref, ubuf_ref):
    tm = u.shape[0]
    width = c1 - c0
    ubuf_ref[0:SUBLANES, 0:width] = carry_ref[:, c0:c1]
    ubuf_ref[SUBLANES:SUBLANES + tm, 0:width] = u
    carry_ref[:, c0:c1] = u[tm - SUBLANES:tm, :]
    u1 = ubuf_ref[SUBLANES - 1:SUBLANES - 1 + tm, 0:width]
    u2 = ubuf_ref[SUBLANES - 2:SUBLANES - 2 + tm, 0:width]
    return (w_ref[0:1, c0:c1] * u2 + w_ref[1:2, c0:c1] * u1) + w_ref[2:3, c0:c1] * u


def _in_proj_kernel(x_ref, g_ref, w_ref, cw_ref, qkv_ref, uc_ref, gate_ref,
                    carry_ref, ubuf_ref):
    @pl.when(pl.program_id(1) == 0)
    def _():
        carry_ref[...] = jnp.zeros_like(carry_ref)

    h = _rms_norm(x_ref[0], g_ref[...]).astype(BF16)

    def proj(c0, c1):
        return jnp.dot(h, w_ref[:, c0:c1], preferred_element_type=F32)

    base = 3 * ATT_WIDTH
    cw = CONV_WIDTH
    cc = proj(base + cw, base + 2 * cw)
    cx = proj(base + 2 * cw, base + 3 * cw)
    y = _causal_conv3(cc * cx, cw_ref, 0, cw, carry_ref, ubuf_ref)
    uc_ref[0] = (proj(base, base + cw) * y).astype(BF16)

    w = ATT_WIDTH
    qkv_ref[0, :, 0:w] = (proj(0, w) * (HEAD_DIM ** -0.5)).astype(BF16)
    qkv_ref[0, :, w:2 * w] = proj(w, 2 * w).astype(BF16)
    qkv_ref[0, :, 2 * w:3 * w] = proj(2 * w, 3 * w).astype(BF16)

    gbase = base + 3 * cw
    for c in range(2 * D_MODEL // cw):
        p = proj(gbase + c * cw, gbase + (c + 1) * cw)
        gate_ref[0, :, c * cw:(c + 1) * cw] = jax.nn.sigmoid(p).astype(BF16)


def _in_proj(x, g, w_in, conv_w):
    b, s, d = x.shape
    tm = TM_PROJ
    const = lambda *_: (0, 0)
    return pl.pallas_call(
        _in_proj_kernel,
        name="in_proj",
        grid=(b, s // tm),
        in_specs=[
            pl.BlockSpec((1, tm, d), lambda i, j: (i, j, 0)),
            pl.BlockSpec((1, d), const),
            pl.BlockSpec((d, IN_COLS), const, pipeline_mode=pl.Buffered(1)),
            pl.BlockSpec((CONV_K, CONV_WIDTH), const),
        ],
        out_specs=[
            pl.BlockSpec((1, tm, 3 * ATT_WIDTH), lambda i, j: (i, j, 0)),
            pl.BlockSpec((1, tm, CONV_WIDTH), lambda i, j: (i, j, 0)),
            pl.BlockSpec((1, tm, 2 * D_MODEL), lambda i, j: (i, j, 0)),
        ],
        out_shape=[
            jax.ShapeDtypeStruct((b, s, 3 * ATT_WIDTH), BF16),
            jax.ShapeDtypeStruct((b, s, CONV_WIDTH), BF16),
            jax.ShapeDtypeStruct((b, s, 2 * D_MODEL), BF16),
        ],
        scratch_shapes=[
            pltpu.VMEM((SUBLANES, CONV_WIDTH), F32),
            pltpu.VMEM((SUBLANES + tm, CONV_WIDTH), F32),
        ],
        compiler_params=pltpu.CompilerParams(
            dimension_semantics=("arbitrary", "arbitrary"),
            vmem_limit_bytes=VMEM_LIMIT_BYTES),
    )(x, g, w_in, conv_w)


def _attn_kernel(q_ref, k_ref, v_ref, tri_ref, o_ref):
    i = pl.program_id(2)
    lane = lax.broadcasted_iota(jnp.int32, (1, LANES), 1)
    row = lax.broadcasted_iota(jnp.int32, (TK, TK), 0)
    col = lax.broadcasted_iota(jnp.int32, (TK, TK), 1)
    strict = col < row
    head0 = lane < HEAD_DIM
    neg_after = tri_ref[...]
    pair_cols = [slice(p * LANES, (p + 1) * LANES) for p in range(PAIRS_PER_STEP)]

    def head_queries(cols):
        q = q_ref[0, :, cols]
        return jnp.where(head0, q, jnp.zeros_like(q)), jnp.where(head0, jnp.zeros_like(q), q)

    qs = [head_queries(cols) for cols in pair_cols]

    def local_terms(j, pair, diagonal):
        start = pl.multiple_of(j * TK, TK)
        ks = k_ref[0, pl.ds(start, TK), pair_cols[pair]]
        vs = v_ref[0, pl.ds(start, TK), pair_cols[pair]]
        zero = jnp.zeros_like(vs)
        v_stack = jnp.concatenate([jnp.where(head0, vs, zero), jnp.where(head0, zero, vs)], axis=0)
        exponents, totals = [], []
        for qh in qs[pair]:
            z = lax.dot_general(qh, ks, (((1,), (1,)), ((), ())), preferred_element_type=F32)
            softplus = jnp.maximum(z, 0.0) + jnp.log(1.0 + jnp.exp2(jnp.abs(z) * (-LOG2_E)))
            if diagonal:
                softplus = jnp.where(strict, softplus, 0.0)
            later = jnp.dot(softplus.astype(BF16), neg_after, preferred_element_type=F32)
            exponents.append((z - softplus) + later)
            totals.append(later[:, 0:1] - softplus[:, 0:1])
        return exponents, totals, v_stack

    def accumulate(terms, acc, runs, diagonal):
        exponents, totals, v_stack = terms
        ws = []
        for e, run in zip(exponents, runs):
            w = jnp.exp(e + run)
            if diagonal:
                w = jnp.where(strict, w, 0.0)
            ws.append(w.astype(BF16))
        acc = acc + jnp.dot(jnp.concatenate(ws, axis=1), v_stack, preferred_element_type=F32)
        return acc, [run + t for run, t in zip(runs, totals)]

    def sweep_block(j, accs, runs, diagonal, offset=None):
        terms = [local_terms(j, p, diagonal) for p in range(PAIRS_PER_STEP)]
        new_accs, new_runs = [], []
        for p in range(PAIRS_PER_STEP):
            pair_runs = runs[p * HEADS_PER_PAIR:(p + 1) * HEADS_PER_PAIR]
            if offset is not None:
                pair_runs = [run + offset for run in pair_runs]
            acc, pair_runs = accumulate(terms[p], accs[p], pair_runs, diagonal)
            new_accs.append(acc)
            new_runs.extend(pair_runs)
        return new_accs, new_runs

    def any_alive(runs):
        alive = jnp.max(functools.reduce(jnp.maximum, runs)) > EXP_ZERO_BELOW
        return alive.astype(jnp.int32)

    accs = [jnp.zeros((TQ, LANES), F32)] * PAIRS_PER_STEP
    runs = [jnp.zeros((TQ, 1), F32)] * (PAIRS_PER_STEP * HEADS_PER_PAIR)
    accs, runs = sweep_block(i, accs, runs, True)
    no_prev = jnp.where(i > 0, 0.0, NO_BLOCK_RUN)
    accs, runs = sweep_block(jnp.maximum(i - 1, 0), accs, runs, False, offset=no_prev)

    n_acc = PAIRS_PER_STEP

    def cond(c):
        return jnp.logical_and(c[0] >= 0, c[1] > 0)

    def body(c):
        j = c[0]
        accs, runs = sweep_block(j, list(c[2:2 + n_acc]), list(c[2 + n_acc:]), False)
        return (j - 1, any_alive(runs), *accs, *runs)

    c = lax.while_loop(cond, body, (i - 2, any_alive(runs), *accs, *runs))
    for p in range(PAIRS_PER_STEP):
        o_ref[0, :, pair_cols[p]] = c[2 + p].astype(o_ref.dtype)


def _attention(qkv):
    b, s, _ = qkv.shape
    assert TQ == TK and s % TQ == 0
    width = PAIRS_PER_STEP * LANES
    n_groups = ATT_WIDTH // width
    key_row = lax.broadcasted_iota(jnp.int32, (TK, TK), 0)
    key_col = lax.broadcasted_iota(jnp.int32, (TK, TK), 1)
    neg_after = jnp.where(key_row > key_col, -1.0, 0.0).astype(BF16)
    return pl.pallas_call(
        _attn_kernel,
        name="attn",
        grid=(b, n_groups, s // TQ),
        in_specs=[
            pl.BlockSpec((1, TQ, width), lambda bi, p, i: (bi, i, p)),
            pl.BlockSpec((1, s, width), lambda bi, p, i: (bi, 0, n_groups + p)),
            pl.BlockSpec((1, s, width), lambda bi, p, i: (bi, 0, 2 * n_groups + p)),
            pl.BlockSpec((TK, TK), lambda bi, p, i: (0, 0)),
        ],
        out_specs=pl.BlockSpec((1, TQ, width), lambda bi, p, i: (bi, i, p)),
        out_shape=jax.ShapeDtypeStruct((b, s, ATT_WIDTH), BF16),
        compiler_params=pltpu.CompilerParams(
            dimension_semantics=("arbitrary", "arbitrary", "arbitrary"),
            vmem_limit_bytes=VMEM_LIMIT_BYTES),
    )(qkv, qkv, qkv, neg_after)


def _mix_out_kernel(o_ref, uc_ref, gate_ref, x_ref, pa_ref, pc_ref, wo_ref, g_ref, out_ref):
    y_att = jnp.dot(o_ref[0], pa_ref[...], preferred_element_type=F32)
    y_conv = jnp.dot(uc_ref[0], pc_ref[...], preferred_element_type=F32)
    g_att = gate_ref[0, :, 0:D_MODEL].astype(F32)
    g_conv = gate_ref[0, :, D_MODEL:2 * D_MODEL].astype(F32)
    merged = (g_att * y_att + g_conv * y_conv).astype(BF16)
    r = jnp.dot(merged, wo_ref[...], preferred_element_type=F32)
    out_ref[0] = x_ref[0] + _rms_norm(r, g_ref[...])


def _mix_out(o, uc, gates, x, p_att, p_conv, w_out, g):
    b, s, d = x.shape
    tm = TM_PROJ
    const = lambda *_: (0, 0)
    tile = lambda width: pl.BlockSpec((1, tm, width), lambda i, j: (i, j, 0))
    return pl.pallas_call(
        _mix_out_kernel,
        name="mix_out",
        grid=(b, s // tm),
        in_specs=[
            tile(ATT_WIDTH), tile(CONV_WIDTH), tile(2 * D_MODEL), tile(d),
            pl.BlockSpec((ATT_WIDTH, d), const),
            pl.BlockSpec((CONV_WIDTH, d), const),
            pl.BlockSpec((d, d), const),
            pl.BlockSpec((1, d), const),
        ],
        out_specs=tile(d),
        out_shape=jax.ShapeDtypeStruct((b, s, d), F32),
        input_output_aliases={3: 0},
        compiler_params=pltpu.CompilerParams(
            dimension_semantics=("arbitrary", "arbitrary"),
            vmem_limit_bytes=VMEM_LIMIT_BYTES),
    )(o, uc, gates, x, p_att, p_conv, w_out, g)


def _ffn_kernel(x_ref, gpre_ref, wup_ref, cw_ref, wdn_ref, gpost_ref, out_ref,
                carry_ref, ubuf_ref, f_ref):
    @pl.when(pl.program_id(1) == 0)
    def _():
        carry_ref[...] = jnp.zeros_like(carry_ref)

    x = x_ref[0]
    h = _rms_norm(x, gpre_ref[...]).astype(BF16)
    for c in range(D_FF // FF_CHUNK):
        halves = []
        for base in (0, D_FF):
            c0 = base + c * FF_CHUNK
            c1 = c0 + FF_CHUNK
            u = jnp.dot(h, wup_ref[:, c0:c1], preferred_element_type=F32)
            halves.append(_causal_conv3(u, cw_ref, c0, c1, carry_ref, ubuf_ref))
        a, g = halves
        f_ref[:, c * FF_CHUNK:(c + 1) * FF_CHUNK] = (
            jax.nn.gelu(g, approximate=True) * a).astype(BF16)
    f = jnp.dot(f_ref[...], wdn_ref[...], preferred_element_type=F32)
    out_ref[0] = x + _rms_norm(f, gpost_ref[...])


def _ffn(x, g_pre, w_up, conv_w, w_down, g_post):
    b, s, d = x.shape
    tm = TM_FFN
    const = lambda *_: (0, 0)
    resident = lambda shape: pl.BlockSpec(shape, const, pipeline_mode=pl.Buffered(1))
    return pl.pallas_call(
        _ffn_kernel,
        name="ffn",
        grid=(b, s // tm),
        in_specs=[
            pl.BlockSpec((1, tm, d), lambda i, j: (i, j, 0)),
            pl.BlockSpec((1, d), const),
            resident((d, 2 * D_FF)),
            pl.BlockSpec((CONV_K, 2 * D_FF), const),
            resident((D_FF, d)),
            pl.BlockSpec((1, d), const),
        ],
        out_specs=pl.BlockSpec((1, tm, d), lambda i, j: (i, j, 0)),
        out_shape=jax.ShapeDtypeStruct((b, s, d), F32),
        input_output_aliases={0: 0},
        scratch_shapes=[
            pltpu.VMEM((SUBLANES, 2 * D_FF), F32),
            pltpu.VMEM((SUBLANES + tm, FF_CHUNK), F32),
            pltpu.VMEM((tm, D_FF), BF16),
        ],
        compiler_params=pltpu.CompilerParams(
            dimension_semantics=("arbitrary", "arbitrary"),
            vmem_limit_bytes=VMEM_LIMIT_BYTES),
    )(x, g_pre, w_up, conv_w, w_down, g_post)


def kernel(x, norm_mix_pre, w_in, conv_mix_w, w_att_branch, w_conv_branch, w_out,
           norm_mix_post, norm_ffn_pre, w_up, conv_ffn_w, w_down, norm_ffn_post):
    depth = w_in.shape[0]
    w_in, w_att_branch, w_conv_branch, w_out, w_up, w_down = (
        w.astype(BF16) for w in (w_in, w_att_branch, w_conv_branch, w_out, w_up, w_down))
    row = lambda g, l: g[l][None, :]
    for l in range(depth):
        qkv, uc, gates = _in_proj(x, row(norm_mix_pre, l), w_in[l], conv_mix_w[l])
        o = _attention(qkv)
        x = _mix_out(o, uc, gates, x, w_att_branch[l], w_conv_branch[l], w_out[l],
                     row(norm_mix_post, l))
        x = _ffn(x, row(norm_ffn_pre, l), w_up[l], conv_ffn_w[l], w_down[l],
                 row(norm_ffn_post, l))
    return x
```

```python
import functools

import jax
import jax.numpy as jnp
from jax import lax
from jax.experimental import pallas as pl
from jax.experimental.pallas import tpu as pltpu

D_MODEL = 1024
ATT_HEADS = 8
HEAD_DIM = 64
ATT_WIDTH = ATT_HEADS * HEAD_DIM
CONV_WIDTH = D_MODEL - ATT_WIDTH
CONV_K = 3
D_FF = 2816
RMS_EPS = 1e-6
IN_COLS = 3 * ATT_WIDTH + 3 * CONV_WIDTH + 2 * D_MODEL

LANES = 128
SUBLANES = 8
VMEM_LIMIT_BYTES = 56 * 1024 * 1024

TM_PROJ = 1024
TM_FFN = 1024
TQ = 256
TK = 256
FF_CHUNK = 256
HEADS_PER_PAIR = LANES // HEAD_DIM
PAIRS_PER_STEP = 4
EXP_ZERO_BELOW = -104.0
NO_BLOCK_RUN = -1e30
LOG2_E = 1.4426950408889634

BF16 = jnp.bfloat16
F32 = jnp.float32


def _rms_norm(x, g):
    var = jnp.mean(x * x, axis=-1, keepdims=True)
    return x * lax.rsqrt(var + RMS_EPS) * g


def _causal_conv3(u, w_ref, c0, c1, carry_ref, ubuf_ref):
    tm = u.shape[0]
    width = c1 - c0
    ubuf_ref[0:SUBLANES, 0:width] = carry_ref[:, c0:c1]
    ubuf_ref[SUBLANES:SUBLANES + tm, 0:width] = u
    carry_ref[:, c0:c1] = u[tm - SUBLANES:tm, :]
    u1 = ubuf_ref[SUBLANES - 1:SUBLANES - 1 + tm, 0:width]
    u2 = ubuf_ref[SUBLANES - 2:SUBLANES - 2 + tm, 0:width]
    return (w_ref[0:1, c0:c1] * u2 + w_ref[1:2, c0:c1] * u1) + w_ref[2:3, c0:c1] * u


def _in_proj_kernel(x_ref, g_ref, w_ref, cw_ref, qkv_ref, uc_ref, gate_ref,
                    carry_ref, ubuf_ref):
    @pl.when(pl.program_id(1) == 0)
    def _():
        carry_ref[...] = jnp.zeros_like(carry_ref)

    h = _rms_norm(x_ref[0], g_ref[...]).astype(BF16)

    def proj(c0, c1):
        return jnp.dot(h, w_ref[:, c0:c1], preferred_element_type=F32)

    base = 3 * ATT_WIDTH
    cw = CONV_WIDTH
    cc = proj(base + cw, base + 2 * cw)
    cx = proj(base + 2 * cw, base + 3 * cw)
    y = _causal_conv3(cc * cx, cw_ref, 0, cw, carry_ref, ubuf_ref)
    uc_ref[0] = (proj(base, base + cw) * y).astype(BF16)

    w = ATT_WIDTH
    qkv_ref[0, :, 0:w] = (proj(0, w) * (HEAD_DIM ** -0.5)).astype(BF16)
    qkv_ref[0, :, w:2 * w] = proj(w, 2 * w).astype(BF16)
    qkv_ref[0, :, 2 * w:3 * w] = proj(2 * w, 3 * w).astype(BF16)

    gbase = base + 3 * cw
    for c in range(2 * D_MODEL // cw):
        p = proj(gbase + c * cw, gbase + (c + 1) * cw)
        gate_ref[0, :, c * cw:(c + 1) * cw] = jax.nn.sigmoid(p).astype(BF16)


def _in_proj(x, g, w_in, conv_w):
    b, s, d = x.shape
    tm = TM_PROJ
    const = lambda *_: (0, 0)
    return pl.pallas_call(
        _in_proj_kernel,
        name="in_proj",
        grid=(b, s // tm),
        in_specs=[
            pl.BlockSpec((1, tm, d), lambda i, j: (i, j, 0)),
            pl.BlockSpec((1, d), const),
            pl.BlockSpec((d, IN_COLS), const, pipeline_mode=pl.Buffered(1)),
            pl.BlockSpec((CONV_K, CONV_WIDTH), const),
        ],
        out_specs=[
            pl.BlockSpec((1, tm, 3 * ATT_WIDTH), lambda i, j: (i, j, 0)),
            pl.BlockSpec((1, tm, CONV_WIDTH), lambda i, j: (i, j, 0)),
            pl.BlockSpec((1, tm, 2 * D_MODEL), lambda i, j: (i, j, 0)),
        ],
        out_shape=[
            jax.ShapeDtypeStruct((b, s, 3 * ATT_WIDTH), BF16),
            jax.ShapeDtypeStruct((b, s, CONV_WIDTH), BF16),
            jax.ShapeDtypeStruct((b, s, 2 * D_MODEL), BF16),
        ],
        scratch_shapes=[
            pltpu.VMEM((SUBLANES, CONV_WIDTH), F32),
            pltpu.VMEM((SUBLANES + tm, CONV_WIDTH), F32),
        ],
        compiler_params=pltpu.CompilerParams(
            dimension_semantics=("arbitrary", "arbitrary"),
            vmem_limit_bytes=VMEM_LIMIT_BYTES),
    )(x, g, w_in, conv_w)


def _attn_kernel(q_ref, k_ref, v_ref, tri_ref, o_ref):
    i = pl.program_id(2)
    lane = lax.broadcasted_iota(jnp.int32, (1, LANES), 1)
    row = lax.broadcasted_iota(jnp.int32, (TK, TK), 0)
    col = lax.broadcasted_iota(jnp.int32, (TK, TK), 1)
    strict = col < row
    head0 = lane < HEAD_DIM
    neg_after = tri_ref[...]
    pair_cols = [slice(p * LANES, (p + 1) * LANES) for p in range(PAIRS_PER_STEP)]

    def head_queries(cols):
        q = q_ref[0, :, cols]
        return jnp.where(head0, q, jnp.zeros_like(q)), jnp.where(head0, jnp.zeros_like(q), q)

    qs = [head_queries(cols) for cols in pair_cols]

    def local_terms(j, pair, diagonal):
        start = pl.multiple_of(j * TK, TK)
        ks = k_ref[0, pl.ds(start, TK), pair_cols[pair]]
        vs = v_ref[0, pl.ds(start, TK), pair_cols[pair]]
        zero = jnp.zeros_like(vs)
        v_stack = jnp.concatenate([jnp.where(head0, vs, zero), jnp.where(head0, zero, vs)], axis=0)
        exponents, totals = [], []
        for qh in qs[pair]:
            z = lax.dot_general(qh, ks, (((1,), (1,)), ((), ())), preferred_element_type=F32)
            softplus = jnp.maximum(z, 0.0) + jnp.log(1.0 + jnp.exp2(jnp.abs(z) * (-LOG2_E)))
            if diagonal:
                softplus = jnp.where(strict, softplus, 0.0)
            later = jnp.dot(softplus.astype(BF16), neg_after, preferred_element_type=F32)
            exponents.append((z - softplus) + later)
            totals.append(later[:, 0:1] - softplus[:, 0:1])
        return exponents, totals, v_stack

    def accumulate(terms, acc, runs, diagonal):
        exponents, totals, v_stack = terms
        ws = []
        for e, run in zip(exponents, runs):
            w = jnp.exp(e + run)
            if diagonal:
                w = jnp.where(strict, w, 0.0)
            ws.append(w.astype(BF16))
        acc = acc + jnp.dot(jnp.concatenate(ws, axis=1), v_stack, preferred_element_type=F32)
        return acc, [run + t for run, t in zip(runs, totals)]

    def sweep_block(j, accs, runs, diagonal, offset=None):
        terms = [local_terms(j, p, diagonal) for p in range(PAIRS_PER_STEP)]
        new_accs, new_runs = [], []
        for p in range(PAIRS_PER_STEP):
            pair_runs = runs[p * HEADS_PER_PAIR:(p + 1) * HEADS_PER_PAIR]
            if offset is not None:
                pair_runs = [run + offset for run in pair_runs]
            acc, pair_runs = accumulate(terms[p], accs[p], pair_runs, diagonal)
            new_accs.append(acc)
            new_runs.extend(pair_runs)
        return new_accs, new_runs

    def any_alive(runs):
        alive = jnp.max(functools.reduce(jnp.maximum, runs)) > EXP_ZERO_BELOW
        return alive.astype(jnp.int32)

    accs = [jnp.zeros((TQ, LANES), F32)] * PAIRS_PER_STEP
    runs = [jnp.zeros((TQ, 1), F32)] * (PAIRS_PER_STEP * HEADS_PER_PAIR)
    accs, runs = sweep_block(i, accs, runs, True)
    no_prev = jnp.where(i > 0, 0.0, NO_BLOCK_RUN)
    accs, runs = sweep_block(jnp.maximum(i - 1, 0), accs, runs, False, offset=no_prev)

    n_acc = PAIRS_PER_STEP

    def cond(c):
        return jnp.logical_and(c[0] >= 0, c[1] > 0)

    def body(c):
        j = c[0]
        accs, runs = sweep_block(j, list(c[2:2 + n_acc]), list(c[2 + n_acc:]), False)
        return (j - 1, any_alive(runs), *accs, *runs)

    c = lax.while_loop(cond, body, (i - 2, any_alive(runs), *accs, *runs))
    for p in range(PAIRS_PER_STEP):
        o_ref[0, :, pair_cols[p]] = c[2 + p].astype(o_ref.dtype)


def _attention(qkv):
    b, s, _ = qkv.shape
    assert TQ == TK and s % TQ == 0
    width = PAIRS_PER_STEP * LANES
    n_groups = ATT_WIDTH // width
    key_row = lax.broadcasted_iota(jnp.int32, (TK, TK), 0)
    key_col = lax.broadcasted_iota(jnp.int32, (TK, TK), 1)
    neg_after = jnp.where(key_row > key_col, -1.0, 0.0).astype(BF16)
    return pl.pallas_call(
        _attn_kernel,
        name="attn",
        grid=(b, n_groups, s // TQ),
        in_specs=[
            pl.BlockSpec((1, TQ, width), lambda bi, p, i: (bi, i, p)),
            pl.BlockSpec((1, s, width), lambda bi, p, i: (bi, 0, n_groups + p)),
            pl.BlockSpec((1, s, width), lambda bi, p, i: (bi, 0, 2 * n_groups + p)),
            pl.BlockSpec((TK, TK), lambda bi, p, i: (0, 0)),
        ],
        out_specs=pl.BlockSpec((1, TQ, width), lambda bi, p, i: (bi, i, p)),
        out_shape=jax.ShapeDtypeStruct((b, s, ATT_WIDTH), BF16),
        compiler_params=pltpu.CompilerParams(
            dimension_semantics=("arbitrary", "arbitrary", "arbitrary"),
            vmem_limit_bytes=VMEM_LIMIT_BYTES),
    )(qkv, qkv, qkv, neg_after)


def _mix_out_kernel(o_ref, uc_ref, gate_ref, x_ref, pa_ref, pc_ref, wo_ref, g_ref, out_ref):
    y_att = jnp.dot(o_ref[0], pa_ref[...], preferred_element_type=F32)
    y_conv = jnp.dot(uc_ref[0], pc_ref[...], preferred_element_type=F32)
    g_att = gate_ref[0, :, 0:D_MODEL].astype(F32)
    g_conv = gate_ref[0, :, D_MODEL:2 * D_MODEL].astype(F32)
    merged = (g_att * y_att + g_conv * y_conv).astype(BF16)
    r = jnp.dot(merged, wo_ref[...], preferred_element_type=F32)
    out_ref[0] = x_ref[0] + _rms_norm(r, g_ref[...])


def _mix_out(o, uc, gates, x, p_att, p_conv, w_out, g, update_in_place):
    b, s, d = x.shape
    tm = TM_PROJ
    const = lambda *_: (0, 0)
    tile = lambda width: pl.BlockSpec((1, tm, width), lambda i, j: (i, j, 0))
    return pl.pallas_call(
        _mix_out_kernel,
        name="mix_out",
        grid=(b, s // tm),
        in_specs=[
            tile(ATT_WIDTH), tile(CONV_WIDTH), tile(2 * D_MODEL), tile(d),
            pl.BlockSpec((ATT_WIDTH, d), const),
            pl.BlockSpec((CONV_WIDTH, d), const),
            pl.BlockSpec((d, d), const),
            pl.BlockSpec((1, d), const),
        ],
        out_specs=tile(d),
        out_shape=jax.ShapeDtypeStruct((b, s, d), F32),
        input_output_aliases={3: 0} if update_in_place else {},
        compiler_params=pltpu.CompilerParams(
            dimension_semantics=("arbitrary", "arbitrary"),
            vmem_limit_bytes=VMEM_LIMIT_BYTES),
    )(o, uc, gates, x, p_att, p_conv, w_out, g)


def _ffn_kernel(x_ref, gpre_ref, wup_ref, cw_ref, wdn_ref, gpost_ref, out_ref,
                carry_ref, ubuf_ref, f_ref):
    @pl.when(pl.program_id(1) == 0)
    def _():
        carry_ref[...] = jnp.zeros_like(carry_ref)

    x = x_ref[0]
    h = _rms_norm(x, gpre_ref[...]).astype(BF16)
    for c in range(D_FF // FF_CHUNK):
        halves = []
        for base in (0, D_FF):
            c0 = base + c * FF_CHUNK
            c1 = c0 + FF_CHUNK
            u = jnp.dot(h, wup_ref[:, c0:c1], preferred_element_type=F32)
            halves.append(_causal_conv3(u, cw_ref, c0, c1, carry_ref, ubuf_ref))
        a, g = halves
        f_ref[:, c * FF_CHUNK:(c + 1) * FF_CHUNK] = (
            jax.nn.gelu(g, approximate=True) * a).astype(BF16)
    f = jnp.dot(f_ref[...], wdn_ref[...], preferred_element_type=F32)
    out_ref[0] = x + _rms_norm(f, gpost_ref[...])


def _ffn(x, g_pre, w_up, conv_w, w_down, g_post):
    b, s, d = x.shape
    tm = TM_FFN
    const = lambda *_: (0, 0)
    resident = lambda shape: pl.BlockSpec(shape, const, pipeline_mode=pl.Buffered(1))
    return pl.pallas_call(
        _ffn_kernel,
        name="ffn",
        grid=(b, s // tm),
        in_specs=[
            pl.BlockSpec((1, tm, d), lambda i, j: (i, j, 0)),
            pl.BlockSpec((1, d), const),
            resident((d, 2 * D_FF)),
            pl.BlockSpec((CONV_K, 2 * D_FF), const),
            resident((D_FF, d)),
            pl.BlockSpec((1, d), const),
        ],
        out_specs=pl.BlockSpec((1, tm, d), lambda i, j: (i, j, 0)),
        out_shape=jax.ShapeDtypeStruct((b, s, d), F32),
        input_output_aliases={0: 0},
        scratch_shapes=[
            pltpu.VMEM((SUBLANES, 2 * D_FF), F32),
            pltpu.VMEM((SUBLANES + tm, FF_CHUNK), F32),
            pltpu.VMEM((tm, D_FF), BF16),
        ],
        compiler_params=pltpu.CompilerParams(
            dimension_semantics=("arbitrary", "arbitrary"),
            vmem_limit_bytes=VMEM_LIMIT_BYTES),
    )(x, g_pre, w_up, conv_w, w_down, g_post)


def kernel(x, norm_mix_pre, w_in, conv_mix_w, w_att_branch, w_conv_branch, w_out,
           norm_mix_post, norm_ffn_pre, w_up, conv_ffn_w, w_down, norm_ffn_post):
    depth = w_in.shape[0]
    w_in, w_att_branch, w_conv_branch, w_out, w_up, w_down = (
        w.astype(BF16) for w in (w_in, w_att_branch, w_conv_branch, w_out, w_up, w_down))
    row = lambda g, l: g[l][None, :]
    for l in range(depth):
        qkv, uc, gates = _in_proj(x, row(norm_mix_pre, l), w_in[l], conv_mix_w[l])
        o = _attention(qkv)
        x = _mix_out(o, uc, gates, x, w_att_branch[l], w_conv_branch[l], w_out[l],
                     row(norm_mix_post, l), update_in_place=l > 0)
        x = _ffn(x, row(norm_ffn_pre, l), w_up[l], conv_ffn_w[l], w_down[l],
                 row(norm_ffn_post, l))
    return x
```

```python
import functools

import jax
import jax.numpy as jnp
from jax import lax
from jax.experimental import pallas as pl
from jax.experimental.pallas import tpu as pltpu

D_MODEL = 1024
ATT_HEADS = 8
HEAD_DIM = 64
ATT_WIDTH = ATT_HEADS * HEAD_DIM
CONV_WIDTH = D_MODEL - ATT_WIDTH
CONV_K = 3
D_FF = 2816
RMS_EPS = 1e-6
IN_COLS = 3 * ATT_WIDTH + 3 * CONV_WIDTH + 2 * D_MODEL

LANES = 128
SUBLANES = 8
VMEM_LIMIT_BYTES = 56 * 1024 * 1024

TM_PROJ = 1024
TM_FFN = 1024
TQ = 256
TK = 256
FF_CHUNK = 256
HEADS_PER_PAIR = LANES // HEAD_DIM
PAIRS_PER_STEP = 4
EXP_ZERO_BELOW = -104.0
NO_BLOCK_RUN = -1e30
LOG2_E = 1.4426950408889634

BF16 = jnp.bfloat16
F32 = jnp.float32


def _rms_norm(x, g):
    var = jnp.mean(x * x, axis=-1, keepdims=True)
    return x * lax.rsqrt(var + RMS_EPS) * g


def _causal_conv3(u, w_ref, c0, c1, carry_ref, ubuf_ref):
    tm = u.shape[0]
    width = c1 - c0
    ubuf_ref[0:SUBLANES, 0:width] = carry_ref[:, c0:c1]
    ubuf_ref[SUBLANES:SUBLANES + tm, 0:width] = u
    carry_ref[:, c0:c1] = u[tm - SUBLANES:tm, :]
    u1 = ubuf_ref[SUBLANES - 1:SUBLANES - 1 + tm, 0:width]
    u2 = ubuf_ref[SUBLANES - 2:SUBLANES - 2 + tm, 0:width]
    return (w_ref[0:1, c0:c1] * u2 + w_ref[1:2, c0:c1] * u1) + w_ref[2:3, c0:c1] * u


def _in_proj_kernel(x_ref, g_ref, w_ref, cw_ref, qkv_ref, uc_ref, gate_ref,
                    carry_ref, ubuf_ref):
    @pl.when(pl.program_id(1) == 0)
    def _():
        carry_ref[...] = jnp.zeros_like(carry_ref)

    h = _rms_norm(x_ref[0], g_ref[...]).astype(BF16)

    def proj(c0, c1):
        return jnp.dot(h, w_ref[:, c0:c1], preferred_element_type=F32)

    base = 3 * ATT_WIDTH
    cw = CONV_WIDTH
    cc = proj(base + cw, base + 2 * cw)
    cx = proj(base + 2 * cw, base + 3 * cw)
    y = _causal_conv3(cc * cx, cw_ref, 0, cw, carry_ref, ubuf_ref)
    uc_ref[0] = (proj(base, base + cw) * y).astype(BF16)

    w = ATT_WIDTH
    qkv_ref[0, :, 0:w] = (proj(0, w) * (HEAD_DIM ** -0.5)).astype(BF16)
    qkv_ref[0, :, w:2 * w] = proj(w, 2 * w).astype(BF16)
    qkv_ref[0, :, 2 * w:3 * w] = proj(2 * w, 3 * w).astype(BF16)

    gbase = base + 3 * cw
    for c in range(2 * D_MODEL // cw):
        p = proj(gbase + c * cw, gbase + (c + 1) * cw)
        gate_ref[0, :, c * cw:(c + 1) * cw] = jax.nn.sigmoid(p).astype(BF16)


def _in_proj(x, g, w_in, conv_w):
    b, s, d = x.shape
    tm = TM_PROJ
    const = lambda *_: (0, 0)
    return pl.pallas_call(
        _in_proj_kernel,
        name="in_proj",
        grid=(b, s // tm),
        in_specs=[
            pl.BlockSpec((1, tm, d), lambda i, j: (i, j, 0)),
            pl.BlockSpec((1, d), const),
            pl.BlockSpec((d, IN_COLS), const, pipeline_mode=pl.Buffered(1)),
            pl.BlockSpec((CONV_K, CONV_WIDTH), const),
        ],
        out_specs=[
            pl.BlockSpec((1, tm, 3 * ATT_WIDTH), lambda i, j: (i, j, 0)),
            pl.BlockSpec((1, tm, CONV_WIDTH), lambda i, j: (i, j, 0)),
            pl.BlockSpec((1, tm, 2 * D_MODEL), lambda i, j: (i, j, 0)),
        ],
        out_shape=[
            jax.ShapeDtypeStruct((b, s, 3 * ATT_WIDTH), BF16),
            jax.ShapeDtypeStruct((b, s, CONV_WIDTH), BF16),
            jax.ShapeDtypeStruct((b, s, 2 * D_MODEL), BF16),
        ],
        scratch_shapes=[
            pltpu.VMEM((SUBLANES, CONV_WIDTH), F32),
            pltpu.VMEM((SUBLANES + tm, CONV_WIDTH), F32),
        ],
        compiler_params=pltpu.CompilerParams(
            dimension_semantics=("arbitrary", "arbitrary"),
            vmem_limit_bytes=VMEM_LIMIT_BYTES),
    )(x, g, w_in, conv_w)


def _attn_kernel(q_ref, k_ref, v_ref, tri_ref, o_ref):
    i = pl.program_id(2)
    lane = lax.broadcasted_iota(jnp.int32, (1, LANES), 1)
    row = lax.broadcasted_iota(jnp.int32, (TK, TK), 0)
    col = lax.broadcasted_iota(jnp.int32, (TK, TK), 1)
    strict = col < row
    head0 = lane < HEAD_DIM
    neg_after = tri_ref[...]
    pair_cols = [slice(p * LANES, (p + 1) * LANES) for p in range(PAIRS_PER_STEP)]

    def head_queries(cols):
        q = q_ref[0, :, cols]
        return jnp.where(head0, q, jnp.zeros_like(q)), jnp.where(head0, jnp.zeros_like(q), q)

    qs = [head_queries(cols) for cols in pair_cols]

    def local_terms(j, pair, diagonal):
        start = pl.multiple_of(j * TK, TK)
        ks = k_ref[0, pl.ds(start, TK), pair_cols[pair]]
        vs = v_ref[0, pl.ds(start, TK), pair_cols[pair]]
        zero = jnp.zeros_like(vs)
        v_stack = jnp.concatenate([jnp.where(head0, vs, zero), jnp.where(head0, zero, vs)], axis=0)
        exponents, totals = [], []
        for qh in qs[pair]:
            z = lax.dot_general(qh, ks, (((1,), (1,)), ((), ())), preferred_element_type=F32)
            softplus = jnp.maximum(z, 0.0) + jnp.log(1.0 + jnp.exp2(jnp.abs(z) * (-LOG2_E)))
            if diagonal:
                softplus = jnp.where(strict, softplus, 0.0)
            later = jnp.dot(softplus.astype(BF16), neg_after, preferred_element_type=F32)
            exponents.append((z - softplus) + later)
            totals.append(later[:, 0:1] - softplus[:, 0:1])
        return exponents, totals, v_stack

    def accumulate(terms, acc, runs, diagonal):
        exponents, totals, v_stack = terms
        ws = []
        for e, run in zip(exponents, runs):
            w = jnp.exp(e + run)
            if diagonal:
                w = jnp.where(strict, w, 0.0)
            ws.append(w.astype(BF16))
        acc = acc + jnp.dot(jnp.concatenate(ws, axis=1), v_stack, preferred_element_type=F32)
        return acc, [run + t for run, t in zip(runs, totals)]

    def sweep_block(j, accs, runs, diagonal, offset=None):
        terms = [local_terms(j, p, diagonal) for p in range(PAIRS_PER_STEP)]
        new_accs, new_runs = [], []
        for p in range(PAIRS_PER_STEP):
            pair_runs = runs[p * HEADS_PER_PAIR:(p + 1) * HEADS_PER_PAIR]
            if offset is not None:
                pair_runs = [run + offset for run in pair_runs]
            acc, pair_runs = accumulate(terms[p], accs[p], pair_runs, diagonal)
            new_accs.append(acc)
            new_runs.extend(pair_runs)
        return new_accs, new_runs

    def any_alive(runs):
        alive = jnp.max(functools.reduce(jnp.maximum, runs)) > EXP_ZERO_BELOW
        return alive.astype(jnp.int32)

    accs = [jnp.zeros((TQ, LANES), F32)] * PAIRS_PER_STEP
    runs = [jnp.zeros((TQ, 1), F32)] * (PAIRS_PER_STEP * HEADS_PER_PAIR)
    accs, runs = sweep_block(i, accs, runs, True)
    no_prev = jnp.where(i > 0, 0.0, NO_BLOCK_RUN)
    accs, runs = sweep_block(jnp.maximum(i - 1, 0), accs, runs, False, offset=no_prev)

    n_acc = PAIRS_PER_STEP

    def cond(c):
        return jnp.logical_and(c[0] >= 0, c[1] > 0)

    def body(c):
        j = c[0]
        accs, runs = sweep_block(j, list(c[2:2 + n_acc]), list(c[2 + n_acc:]), False)
        return (j - 1, any_alive(runs), *accs, *runs)

    c = lax.while_loop(cond, body, (i - 2, any_alive(runs), *accs, *runs))
    for p in range(PAIRS_PER_STEP):
        o_ref[0, :, pair_cols[p]] = c[2 + p].astype(o_ref.dtype)


def _attention(qkv):
    b, s, _ = qkv.shape
    assert TQ == TK and s % TQ == 0
    width = PAIRS_PER_STEP * LANES
    n_groups = ATT_WIDTH // width
    key_row = lax.broadcasted_iota(jnp.int32, (TK, TK), 0)
    key_col = lax.broadcasted_iota(jnp.int32, (TK, TK), 1)
    neg_after = jnp.where(key_row > key_col, -1.0, 0.0).astype(BF16)
    return pl.pallas_call(
        _attn_kernel,
        name="attn",
        grid=(b, n_groups, s // TQ),
        in_specs=[
            pl.BlockSpec((1, TQ, width), lambda bi, p, i: (bi, i, p)),
            pl.BlockSpec((1, s, width), lambda bi, p, i: (bi, 0, n_groups + p)),
            pl.BlockSpec((1, s, width), lambda bi, p, i: (bi, 0, 2 * n_groups + p)),
            pl.BlockSpec((TK, TK), lambda bi, p, i: (0, 0)),
        ],
        out_specs=pl.BlockSpec((1, TQ, width), lambda bi, p, i: (bi, i, p)),
        out_shape=jax.ShapeDtypeStruct((b, s, ATT_WIDTH), BF16),
        compiler_params=pltpu.CompilerParams(
            dimension_semantics=("arbitrary", "arbitrary", "arbitrary"),
            vmem_limit_bytes=VMEM_LIMIT_BYTES),
    )(qkv, qkv, qkv, neg_after)


def _mix_out_kernel(o_ref, uc_ref, gate_ref, x_ref, pa_ref, pc_ref, wo_ref, g_ref, out_ref):
    half = o_ref.shape[1] // 2
    for rows in (slice(0, half), slice(half, 2 * half)):
        y_att = jnp.dot(o_ref[0, rows, :], pa_ref[...], preferred_element_type=F32)
        y_conv = jnp.dot(uc_ref[0, rows, :], pc_ref[...], preferred_element_type=F32)
        g_att = gate_ref[0, rows, 0:D_MODEL].astype(F32)
        g_conv = gate_ref[0, rows, D_MODEL:2 * D_MODEL].astype(F32)
        merged = (g_att * y_att + g_conv * y_conv).astype(BF16)
        r = jnp.dot(merged, wo_ref[...], preferred_element_type=F32)
        out_ref[0, rows, :] = x_ref[0, rows, :] + _rms_norm(r, g_ref[...])


def _mix_out(o, uc, gates, x, p_att, p_conv, w_out, g, update_in_place):
    b, s, d = x.shape
    tm = TM_PROJ
    const = lambda *_: (0, 0)
    tile = lambda width: pl.BlockSpec((1, tm, width), lambda i, j: (i, j, 0))
    return pl.pallas_call(
        _mix_out_kernel,
        name="mix_out",
        grid=(b, s // tm),
        in_specs=[
            tile(ATT_WIDTH), tile(CONV_WIDTH), tile(2 * D_MODEL), tile(d),
            pl.BlockSpec((ATT_WIDTH, d), const),
            pl.BlockSpec((CONV_WIDTH, d), const),
            pl.BlockSpec((d, d), const),
            pl.BlockSpec((1, d), const),
        ],
        out_specs=tile(d),
        out_shape=jax.ShapeDtypeStruct((b, s, d), F32),
        input_output_aliases={3: 0} if update_in_place else {},
        compiler_params=pltpu.CompilerParams(
            dimension_semantics=("arbitrary", "arbitrary"),
            vmem_limit_bytes=VMEM_LIMIT_BYTES),
    )(o, uc, gates, x, p_att, p_conv, w_out, g)


def _ffn_kernel(x_ref, gpre_ref, wup_ref, cw_ref, wdn_ref, gpost_ref, out_ref,
                carry_ref, ubuf_ref, f_ref):
    @pl.when(pl.program_id(1) == 0)
    def _():
        carry_ref[...] = jnp.zeros_like(carry_ref)

    x = x_ref[0]
    h = _rms_norm(x, gpre_ref[...]).astype(BF16)
    for c in range(D_FF // FF_CHUNK):
        halves = []
        for base in (0, D_FF):
            c0 = base + c * FF_CHUNK
            c1 = c0 + FF_CHUNK
            u = jnp.dot(h, wup_ref[:, c0:c1], preferred_element_type=F32)
            halves.append(_causal_conv3(u, cw_ref, c0, c1, carry_ref, ubuf_ref))
        a, g = halves
        f_ref[:, c * FF_CHUNK:(c + 1) * FF_CHUNK] = (
            jax.nn.gelu(g, approximate=True) * a).astype(BF16)
    f = jnp.dot(f_ref[...], wdn_ref[...], preferred_element_type=F32)
    out_ref[0] = x + _rms_norm(f, gpost_ref[...])


def _ffn(x, g_pre, w_up, conv_w, w_down, g_post):
    b, s, d = x.shape
    tm = TM_FFN
    const = lambda *_: (0, 0)
    resident = lambda shape: pl.BlockSpec(shape, const, pipeline_mode=pl.Buffered(1))
    return pl.pallas_call(
        _ffn_kernel,
        name="ffn",
        grid=(b, s // tm),
        in_specs=[
            pl.BlockSpec((1, tm, d), lambda i, j: (i, j, 0)),
            pl.BlockSpec((1, d), const),
            resident((d, 2 * D_FF)),
            pl.BlockSpec((CONV_K, 2 * D_FF), const),
            resident((D_FF, d)),
            pl.BlockSpec((1, d), const),
        ],
        out_specs=pl.BlockSpec((1, tm, d), lambda i, j: (i, j, 0)),
        out_shape=jax.ShapeDtypeStruct((b, s, d), F32),
        input_output_aliases={0: 0},
        scratch_shapes=[
            pltpu.VMEM((SUBLANES, 2 * D_FF), F32),
            pltpu.VMEM((SUBLANES + tm, FF_CHUNK), F32),
            pltpu.VMEM((tm, D_FF), BF16),
        ],
        compiler_params=pltpu.CompilerParams(
            dimension_semantics=("arbitrary", "arbitrary"),
            vmem_limit_bytes=VMEM_LIMIT_BYTES),
    )(x, g_pre, w_up, conv_w, w_down, g_post)


def kernel(x, norm_mix_pre, w_in, conv_mix_w, w_att_branch, w_conv_branch, w_out,
           norm_mix_post, norm_ffn_pre, w_up, conv_ffn_w, w_down, norm_ffn_post):
    depth = w_in.shape[0]
    w_in, w_att_branch, w_conv_branch, w_out, w_up, w_down = (
        w.astype(BF16) for w in (w_in, w_att_branch, w_conv_branch, w_out, w_up, w_down))
    row = lambda g, l: g[l][None, :]
    for l in range(depth):
        qkv, uc, gates = _in_proj(x, row(norm_mix_pre, l), w_in[l], conv_mix_w[l])
        o = _attention(qkv)
        x = _mix_out(o, uc, gates, x, w_att_branch[l], w_conv_branch[l], w_out[l],
                     row(norm_mix_post, l), update_in_place=l > 0)
        x = _ffn(x, row(norm_ffn_pre, l), w_up[l], conv_ffn_w[l], w_down[l],
                 row(norm_ffn_post, l))
    return x
```
